```python
import math
import jax
import jax.numpy as jnp
from jax import lax
import numpy as np


D_MODEL = 1024
BATCH = 4
SEQ = 4096
DEPTH = 2

GRID_W = 64
CTX_LEN = 256
N_EVEN = (DEPTH + 1) // 2
N_ODD = DEPTH // 2
RMS_EPS = 1e-6
N_MOD = 6

MIX_W = D_MODEL
GLA_W = MIX_W // 2
GLA_HEADS = 4
GLA_DV_HEAD = GLA_W // GLA_HEADS
GLA_DK_HEAD = GLA_DV_HEAD // 2
GLA_KEY_W = GLA_HEADS * GLA_DK_HEAD
GLA_GATE_RANK = 16
GLA_GATE_NORM = 16.0
GLA_CHUNK = 64
S5_W = MIX_W - GLA_W
S5_GROUP_CH = 16
S5_GROUPS = S5_W // S5_GROUP_CH
S5_STATE = 64
S5_DT_MIN = 0.001
S5_DT_MAX = 0.1
REC_IN_SPLITS = (GLA_KEY_W, GLA_KEY_W, GLA_W, GLA_W, GLA_GATE_RANK, GLA_GATE_RANK, S5_W)
REC_IN_W = 2 * GLA_KEY_W + 2 * GLA_W + 2 * GLA_GATE_RANK + S5_W
MLA_HEADS = 8
MLA_Q_RANK = 384
MLA_KV_RANK = 256
MLA_NOPE = 128
MLA_ROPE = 64
MLA_V = 128
ROPE_THETA = 10000.0
Q_BLOCK = 128
N_EXPERTS = 32
TOP_K = 4
D_EXPERT = D_MODEL
SWIGLU_LIMIT = 7.0
SWIGLU_ALPHA = 1.702
MOE_BLOCK = 128

kernel_name = 'hybrid_gla_s5_mla_moe_prefix_dit'


def rmsnorm(t, g):
    t32 = t.astype(jnp.float32)
    y = t32 * lax.rsqrt(jnp.mean(t32 * t32, axis=-1, keepdims=True) + RMS_EPS)
    return (y * g.astype(jnp.float32)).astype(t.dtype)


def modulate(t, g, shift, scale):
    return rmsnorm(t, g) * (1.0 + scale) + shift


def split_widths(t, widths):
    out = []
    start = 0
    for w in widths:
        out.append(t[..., start:start + w])
        start += w
    return out


def to_heads(t, n_heads):
    b, l, w = t.shape
    return t.reshape(b, l, n_heads, w // n_heads).transpose(0, 2, 1, 3)


def gla_chunked(q, k, v, log_g, s0):
    q, k, v, log_g = (t.astype(jnp.float32) for t in (q, k, v, log_g))
    b, h, l, dk = q.shape
    dv = v.shape[-1]
    n = l // GLA_CHUNK
    q, k, v, log_g = (t.reshape(b, h, n, GLA_CHUNK, t.shape[-1]) for t in (q, k, v, log_g))
    cum = jnp.cumsum(log_g, axis=3)
    cum_last = cum[:, :, :, -1:, :]
    q_dec = q * jnp.exp(cum)
    k_inv = k * jnp.exp(-cum)
    k_end = k * jnp.exp(cum_last - cum)
    mask = jnp.tril(jnp.ones((GLA_CHUNK, GLA_CHUNK), bool))
    att = jnp.where(mask, jnp.einsum('bhnid,bhnjd->bhnij', q_dec, k_inv), 0.0)
    o_intra = jnp.einsum('bhnij,bhnjv->bhniv', att, v)

    def step(s, inp):
        qd, ke, vc, dl = inp
        o = jnp.einsum('bhid,bhdv->bhiv', qd, s)
        s = dl[..., None] * s + jnp.einsum('bhjd,bhjv->bhdv', ke, vc)
        return s, o

    xs = (q_dec.transpose(2, 0, 1, 3, 4), k_end.transpose(2, 0, 1, 3, 4),
          v.transpose(2, 0, 1, 3, 4), jnp.exp(cum_last[:, :, :, 0, :]).transpose(2, 0, 1, 3))
    s_fin, o_inter = lax.scan(step, s0.astype(jnp.float32), xs)
    o = o_intra + o_inter.transpose(1, 2, 0, 3, 4)
    return o.reshape(b, h, l, dv), s_fin


def _linear_combine(e1, e2):
    a1, b1 = e1
    a2, b2 = e2
    return a1 * a2, a2 * b1 + b2


def s5_scan(u, lam_re, lam_im, log_dt, b_re, b_im, c_re, c_im, h0, reverse):
    f32 = jnp.float32
    lam = lax.complex(lam_re.astype(f32), lam_im.astype(f32))
    dt = jnp.exp(log_dt.astype(f32))[:, None]
    a_bar = jnp.exp(lam * dt)
    b_bar = ((a_bar - 1.0) / lam)[..., None] * lax.complex(b_re.astype(f32), b_im.astype(f32))
    bu = jnp.einsum('gph,blgh->blgp', b_bar, u.astype(f32))
    first = -1 if reverse else 0
    bu = bu.at[:, first].add(a_bar * h0)
    a = jnp.broadcast_to(a_bar, (1,) + bu.shape[1:])
    _, hs = lax.associative_scan(_linear_combine, (a, bu), reverse=reverse, axis=1)
    cmat = lax.complex(c_re.astype(f32), c_im.astype(f32))
    y = jnp.real(jnp.einsum('ghp,blgp->blgh', cmat, hs))
    h_fin = hs[:, 0] if reverse else hs[:, -1]
    return y, h_fin


def ab_mixer(hx, hz, w_in, w_out, gate_w, gate_b, gla_g, lam_re, lam_im, log_dt,
             b_re, b_im, c_re, c_im, s5_d, glu_w, glu_b, ctx_out):
    px = split_widths(hx @ w_in, REC_IN_SPLITS)
    pz = split_widths(hz @ w_in, REC_IN_SPLITS)

    def gla_inputs(p):
        q, k, v, _, rf, rb, _ = p
        gf = jax.nn.log_sigmoid((rf @ gate_w[0] + gate_b[0]).astype(jnp.float32)) / GLA_GATE_NORM
        gb = jax.nn.log_sigmoid((rb @ gate_w[1] + gate_b[1]).astype(jnp.float32)) / GLA_GATE_NORM
        return (to_heads(q * GLA_DK_HEAD ** -0.5, GLA_HEADS), to_heads(k, GLA_HEADS),
                to_heads(v, GLA_HEADS), to_heads(gf, GLA_HEADS), to_heads(gb, GLA_HEADS))

    def flip(t):
        return jnp.flip(t, axis=2)

    qz, kz, vz, gfz, gbz = gla_inputs(pz)
    qx, kx, vx, gfx, gbx = gla_inputs(px)
    b = hx.shape[0]
    s0 = jnp.zeros((b, GLA_HEADS, GLA_DK_HEAD, GLA_DV_HEAD), jnp.float32)
    oz_f, sz_f = gla_chunked(qz, kz, vz, gfz, s0)
    oz_b, sz_b = gla_chunked(flip(qz), flip(kz), flip(vz), flip(gbz), s0)
    ox_f, _ = gla_chunked(qx, kx, vx, gfx, sz_f)
    ox_b, _ = gla_chunked(flip(qx), flip(kx), flip(vx), flip(gbx), sz_b)

    def gla_out(o, g):
        bb, hh, ll, dv = o.shape
        o = rmsnorm(o, gla_g).transpose(0, 2, 1, 3).reshape(bb, ll, hh * dv)
        return o.astype(g.dtype) * jax.nn.silu(g)

    def s5_in(p):
        u = p[6]
        return u.reshape(u.shape[0], u.shape[1], S5_GROUPS, S5_GROUP_CH)

    uz, ux = s5_in(pz), s5_in(px)
    h_zero = jnp.zeros((b, S5_GROUPS, S5_STATE), jnp.complex64)
    fwd = (lam_re[0], lam_im[0], log_dt[0], b_re[0], b_im[0], c_re[0], c_im[0])
    bwd = (lam_re[1], lam_im[1], log_dt[1], b_re[1], b_im[1], c_re[1], c_im[1])
    yz_f, hz_f = s5_scan(uz, *fwd, h_zero, False)
    yz_b, hz_b = s5_scan(uz, *bwd, h_zero, True)
    yx_f, _ = s5_scan(ux, *fwd, hz_f, False)
    yx_b, _ = s5_scan(ux, *bwd, hz_b, True)

    def s5_out(yf, yb, u_flat):
        bb, ll = u_flat.shape[0], u_flat.shape[1]
        y = (yf + yb).reshape(bb, ll, S5_W).astype(u_flat.dtype) + s5_d * u_flat
        y = jax.nn.gelu(y)
        return y * jax.nn.sigmoid(y @ glu_w + glu_b)

    yx = jnp.concatenate([gla_out(ox_f + flip(ox_b), px[3]), s5_out(yx_f, yx_b, px[6])], axis=-1) @ w_out
    if ctx_out:
        yz = jnp.concatenate([gla_out(oz_f + flip(oz_b), pz[3]), s5_out(yz_f, yz_b, pz[6])], axis=-1) @ w_out
    else:
        yz = None
    return yx, yz


def axial_rope_tables(seq_len):
    rows = seq_len // GRID_W
    f32 = jnp.float32
    row = jnp.broadcast_to(jnp.arange(rows, dtype=f32)[:, None], (rows, GRID_W)).reshape(-1)
    col = jnp.broadcast_to(jnp.arange(GRID_W, dtype=f32)[None, :], (rows, GRID_W)).reshape(-1)
    n_freq = MLA_ROPE // 4
    inv = ROPE_THETA ** (-jnp.arange(n_freq, dtype=f32) / n_freq)
    ang_r = row[:, None] * inv
    ang_c = col[:, None] * inv
    return (jnp.cos(ang_r)[:, None, :], jnp.sin(ang_r)[:, None, :],
            jnp.cos(ang_c)[:, None, :], jnp.sin(ang_c)[:, None, :])


def axial_rope(t, tabs):
    cos_r, sin_r, cos_c, sin_c = tabs
    t32 = t.astype(jnp.float32)
    half = MLA_ROPE // 2
    quarter = half // 2

    def rot(u, cs, sn):
        u1, u2 = u[..., :quarter], u[..., quarter:]
        return jnp.concatenate([u1 * cs - u2 * sn, u2 * cs + u1 * sn], axis=-1)

    out = jnp.concatenate([rot(t32[..., :half], cos_r, sin_r), rot(t32[..., half:], cos_c, sin_c)], axis=-1)
    return out.astype(t.dtype)


def block_attention(q, k, v):
    b, lq, h, dq = q.shape
    nb = lq // Q_BLOCK
    qb = q.reshape(b, nb, Q_BLOCK, h, dq).transpose(1, 0, 2, 3, 4)
    scale = dq ** -0.5

    def one(qi):
        s = jnp.einsum('bqhd,bkhd->bhqk', qi, k).astype(jnp.float32) * scale
        p = jax.nn.softmax(s, axis=-1).astype(v.dtype)
        return jnp.einsum('bhqk,bkhv->bqhv', p, v)

    o = lax.map(one, qb)
    return o.transpose(1, 0, 2, 3, 4).reshape(b, lq, h, v.shape[-1])


def mla_mixer(hx, hz, w_down, q_norm, kv_norm, w_uq, w_ukv, w_o, rope, ctx_out):
    def project(h, tabs):
        b, l, _ = h.shape
        cq, ckv, kr = split_widths(h @ w_down, (MLA_Q_RANK, MLA_KV_RANK, MLA_ROPE))
        q = (rmsnorm(cq, q_norm) @ w_uq).reshape(b, l, MLA_HEADS, MLA_NOPE + MLA_ROPE)
        kv = (rmsnorm(ckv, kv_norm) @ w_ukv).reshape(b, l, MLA_HEADS, MLA_NOPE + MLA_V)
        q_nope, q_rope = q[..., :MLA_NOPE], q[..., MLA_NOPE:]
        k_nope, v = kv[..., :MLA_NOPE], kv[..., MLA_NOPE:]
        kr = kr[:, :, None, :]
        if tabs is not None:
            q_rope = axial_rope(q_rope, tabs)
            kr = axial_rope(kr, tabs)
        q = jnp.concatenate([q_nope, q_rope], axis=-1)
        k = jnp.concatenate([k_nope, jnp.broadcast_to(kr, (b, l, MLA_HEADS, MLA_ROPE))], axis=-1)
        return q, k, v

    qz, kz, vz = project(hz, None)
    qx, kx, vx = project(hx, rope)
    ox = block_attention(qx, jnp.concatenate([kz, kx], axis=1), jnp.concatenate([vz, vx], axis=1))
    yx = ox.reshape(hx.shape[0], hx.shape[1], MLA_HEADS * MLA_V) @ w_o
    if ctx_out:
        oz = block_attention(qz, kz, vz)
        yz = oz.reshape(hz.shape[0], hz.shape[1], MLA_HEADS * MLA_V) @ w_o
    else:
        yz = None
    return yx, yz


def moe(t, router_w, router_b, w_in, b_in, w_out, b_out):
    n_tok, d = t.shape
    logits = (t @ router_w + router_b).astype(jnp.float32)
    top_val, top_idx = lax.top_k(logits, TOP_K)
    gates = jax.nn.softmax(top_val, axis=-1)
    n_asg = n_tok * TOP_K
    flat_e = top_idx.reshape(-1)
    order = jnp.argsort(flat_e)
    sorted_e = flat_e[order]
    counts = jnp.bincount(flat_e, length=N_EXPERTS)
    padded = (counts + MOE_BLOCK - 1) // MOE_BLOCK * MOE_BLOCK
    pad_end = jnp.cumsum(padded)
    pad_start = pad_end - padded
    grp_start = jnp.cumsum(counts) - counts
    dest_sorted = pad_start[sorted_e] + jnp.arange(n_asg) - grp_start[sorted_e]
    n_blocks = -(-n_asg // MOE_BLOCK) + N_EXPERTS
    n_rows = n_blocks * MOE_BLOCK
    row_tok = jnp.full((n_rows,), n_tok, jnp.int32).at[dest_sorted].set((order // TOP_K).astype(jnp.int32))
    t_ext = jnp.concatenate([t, jnp.zeros((1, d), t.dtype)], axis=0)
    rows = t_ext[row_tok].reshape(n_blocks, MOE_BLOCK, d)
    blk_e = jnp.minimum(jnp.searchsorted(pad_end, jnp.arange(n_blocks) * MOE_BLOCK, side='right'), N_EXPERTS - 1)

    def expert_block(args):
        xb, e = args
        gu = xb @ w_in[e] + b_in[e]
        gate, up = gu[:, :D_EXPERT], gu[:, D_EXPERT:]
        gate = jnp.minimum(gate, SWIGLU_LIMIT)
        up = jnp.clip(up, -SWIGLU_LIMIT, SWIGLU_LIMIT)
        act = (up + 1.0) * gate * jax.nn.sigmoid(SWIGLU_ALPHA * gate)
        return act @ w_out[e] + b_out[e]

    y_rows = lax.map(expert_block, (rows, blk_e)).reshape(n_rows, d)
    dest = jnp.zeros((n_asg,), dest_sorted.dtype).at[order].set(dest_sorted)
    y_asg = y_rows[dest].reshape(n_tok, TOP_K, d)
    return jnp.einsum('tk,tkd->td', gates.astype(t.dtype), y_asg)


def setup_inputs(seed: int = 0) -> dict:
    key = jax.random.key(seed)
    ks = iter(jax.random.split(key, 64))
    f32 = jnp.float32
    D = D_MODEL

    def nrm(shape, scale):
        return scale * jax.random.normal(next(ks), shape, f32)

    def gain(shape):
        return 1.0 + nrm(shape, 0.02)

    inp = {}
    inp['x'] = nrm((BATCH, SEQ, D), 1.0)
    inp['c'] = nrm((BATCH, D), 1.0)
    inp['ctx'] = nrm((BATCH, CTX_LEN, D), 1.0)
    inp['c_ctx'] = nrm((D,), 1.0)
    inp['mod_w'] = nrm((DEPTH, D, N_MOD * D), 0.5 * D ** -0.5)
    inp['mod_b'] = nrm((DEPTH, N_MOD * D), 0.02)
    inp['norm_mix_pre'] = gain((DEPTH, D))
    inp['norm_mix_post'] = gain((DEPTH, D))
    inp['norm_ffn_pre'] = gain((DEPTH, D))
    inp['norm_ffn_post'] = gain((DEPTH, D))
    inp['router_w'] = nrm((DEPTH, D, N_EXPERTS), D ** -0.5)
    inp['router_b'] = nrm((DEPTH, N_EXPERTS), 0.01)
    inp['exp_w_in'] = nrm((DEPTH, N_EXPERTS, D, 2 * D_EXPERT), D ** -0.5)
    inp['exp_b_in'] = nrm((DEPTH, N_EXPERTS, 2 * D_EXPERT), 0.01)
    inp['exp_w_out'] = nrm((DEPTH, N_EXPERTS, D_EXPERT, D), D_EXPERT ** -0.5)
    inp['exp_b_out'] = nrm((DEPTH, N_EXPERTS, D), 0.01)
    inp['rec_w_in'] = nrm((N_EVEN, D, REC_IN_W), D ** -0.5)
    inp['rec_w_out'] = nrm((N_EVEN, MIX_W, D), MIX_W ** -0.5)
    inp['gla_gate_w'] = nrm((N_EVEN, 2, GLA_GATE_RANK, GLA_KEY_W), GLA_GATE_RANK ** -0.5)
    inp['gla_gate_b'] = nrm((N_EVEN, 2, GLA_KEY_W), 0.02)
    inp['gla_norm'] = gain((N_EVEN, GLA_DV_HEAD))
    s5_shape = (N_EVEN, 2, S5_GROUPS, S5_STATE)
    inp['s5_lam_re'] = -0.5 + nrm(s5_shape, 0.01)
    inp['s5_lam_im'] = math.pi * jnp.arange(S5_STATE, dtype=f32) + nrm(s5_shape, 0.01)
    inp['s5_log_dt'] = jax.random.uniform(next(ks), (N_EVEN, 2, S5_GROUPS), f32,
                                          minval=math.log(S5_DT_MIN), maxval=math.log(S5_DT_MAX))
    inp['s5_b_re'] = nrm((N_EVEN, 2, S5_GROUPS, S5_STATE, S5_GROUP_CH), (2 * S5_GROUP_CH) ** -0.5)
    inp['s5_b_im'] = nrm((N_EVEN, 2, S5_GROUPS, S5_STATE, S5_GROUP_CH), (2 * S5_GROUP_CH) ** -0.5)
    inp['s5_c_re'] = nrm((N_EVEN, 2, S5_GROUPS, S5_GROUP_CH, S5_STATE), S5_STATE ** -0.5)
    inp['s5_c_im'] = nrm((N_EVEN, 2, S5_GROUPS, S5_GROUP_CH, S5_STATE), S5_STATE ** -0.5)
    inp['s5_d'] = nrm((N_EVEN, S5_W), 1.0)
    inp['s5_glu_w'] = nrm((N_EVEN, S5_W, S5_W), S5_W ** -0.5)
    inp['s5_glu_b'] = nrm((N_EVEN, S5_W), 0.02)
    inp['mla_w_down'] = nrm((N_ODD, D, MLA_Q_RANK + MLA_KV_RANK + MLA_ROPE), D ** -0.5)
    inp['mla_q_norm'] = gain((N_ODD, MLA_Q_RANK))
    inp['mla_kv_norm'] = gain((N_ODD, MLA_KV_RANK))
    inp['mla_w_uq'] = nrm((N_ODD, MLA_Q_RANK, MLA_HEADS * (MLA_NOPE + MLA_ROPE)), MLA_Q_RANK ** -0.5)
    inp['mla_w_ukv'] = nrm((N_ODD, MLA_KV_RANK, MLA_HEADS * (MLA_NOPE + MLA_V)), MLA_KV_RANK ** -0.5)
    inp['mla_w_o'] = nrm((N_ODD, MLA_HEADS * MLA_V, D), (MLA_HEADS * MLA_V) ** -0.5)
    return inp


def reference(x, c, ctx, c_ctx, mod_w, mod_b, norm_mix_pre, norm_mix_post, norm_ffn_pre, norm_ffn_post,
              router_w, router_b, exp_w_in, exp_b_in, exp_w_out, exp_b_out, rec_w_in, rec_w_out,
              gla_gate_w, gla_gate_b, gla_norm, s5_lam_re, s5_lam_im, s5_log_dt, s5_b_re, s5_b_im,
              s5_c_re, s5_c_im, s5_d, s5_glu_w, s5_glu_b, mla_w_down, mla_q_norm, mla_kv_norm,
              mla_w_uq, mla_w_ukv, mla_w_o):
    seq_len = x.shape[1]
    d = x.shape[2]
    rope = axial_rope_tables(seq_len)
    z = ctx
    c_act = jax.nn.silu(c)
    cc_act = jax.nn.silu(c_ctx)
    for i in range(DEPTH):
        last = i == DEPTH - 1
        j = i // 2
        mods_x = jnp.split((c_act @ mod_w[i] + mod_b[i])[:, None, :], N_MOD, axis=-1)
        mods_z = jnp.split((cc_act @ mod_w[i] + mod_b[i])[None, None, :], N_MOD, axis=-1)
        hx = modulate(x, norm_mix_pre[i], mods_x[0], mods_x[1])
        hz = modulate(z, norm_mix_pre[i], mods_z[0], mods_z[1])
        if i % 2 == 0:
            yx, yz = ab_mixer(hx, hz, rec_w_in[j], rec_w_out[j], gla_gate_w[j], gla_gate_b[j], gla_norm[j],
                              s5_lam_re[j], s5_lam_im[j], s5_log_dt[j], s5_b_re[j], s5_b_im[j],
                              s5_c_re[j], s5_c_im[j], s5_d[j], s5_glu_w[j], s5_glu_b[j], not last)
        else:
            yx, yz = mla_mixer(hx, hz, mla_w_down[j], mla_q_norm[j], mla_kv_norm[j], mla_w_uq[j],
                               mla_w_ukv[j], mla_w_o[j], rope, not last)
        x = x + mods_x[2] * rmsnorm(yx, norm_mix_post[i])
        fx_in = modulate(x, norm_ffn_pre[i], mods_x[3], mods_x[4]).reshape(-1, d)
        if last:
            fx = moe(fx_in, router_w[i], router_b[i], exp_w_in[i], exp_b_in[i], exp_w_out[i], exp_b_out[i])
        else:
            z = z + mods_z[2] * rmsnorm(yz, norm_mix_post[i])
            fz_in = modulate(z, norm_ffn_pre[i], mods_z[3], mods_z[4]).reshape(-1, d)
            f = moe(jnp.concatenate([fx_in, fz_in], axis=0), router_w[i], router_b[i],
                    exp_w_in[i], exp_b_in[i], exp_w_out[i], exp_b_out[i])
            n_x = fx_in.shape[0]
            fx = f[:n_x]
            z = z + mods_z[5] * rmsnorm(f[n_x:].reshape(z.shape), norm_ffn_post[i])
        x = x + mods_x[5] * rmsnorm(fx.reshape(x.shape), norm_ffn_post[i])
    return x
```

```python
import functools
import math

import jax
import jax.numpy as jnp
from jax import lax
from jax.experimental import pallas as pl
from jax.experimental.pallas import tpu as pltpu

F32 = jnp.float32
BF16 = jnp.bfloat16

RMS_EPS = 1e-6
N_MOD = 6
GLA_HEADS = 4
GLA_GATE_RANK = 16
GLA_GATE_NORM = 16.0
GLA_CHUNK = 64
S5_GROUP_CH = 16
S5_STATE = 64
S5_CHUNK = 16
MLA_HEADS = 8
MLA_Q_RANK = 384
MLA_KV_RANK = 256
MLA_NOPE = 128
MLA_ROPE = 64
MLA_V = 128
MLA_QK_PAD = 256
ROPE_THETA = 10000.0
GRID_W = 64
TOP_K = 4
SWIGLU_LIMIT = 7.0
SWIGLU_ALPHA = 1.702
LOG2E = 1.4426950408889634

ROW_BLOCK = 256
ATTN_Q_BLOCK = 512
MOE_ROWS = 256
LANES = 128
VMEM_LIMIT = 56 * 1024 * 1024


def _cparams(*sem):
    return pltpu.CompilerParams(dimension_semantics=sem, vmem_limit_bytes=VMEM_LIMIT)


def _dot(a, b):
    return jnp.dot(a, b, preferred_element_type=F32)


def _dot_nt(a, b):
    return lax.dot_general(a, b, (((1,), (1,)), ((), ())), preferred_element_type=F32)


def _split3(a):
    hi = a.astype(BF16)
    r1 = a - hi.astype(F32)
    mid = r1.astype(BF16)
    lo = (r1 - mid.astype(F32)).astype(BF16)
    return hi, mid, lo


def _dot_f32(a, b):
    a_hi, a_mid, _ = _split3(a)
    b_hi, b_mid, _ = _split3(b)
    return _dot(a_hi, b_hi) + _dot(a_mid, b_hi) + _dot(a_hi, b_mid)


def _rms(x, g):
    return x * lax.rsqrt(jnp.mean(x * x, axis=-1, keepdims=True) + RMS_EPS) * g


def _sigmoid(x):
    return 1.0 / (1.0 + jnp.exp(-x))


def _mod_parts(m, d):
    return [m[:, i * d:(i + 1) * d] for i in range(N_MOD)]


def _mods_kernel(a_ref, w_ref, b_ref, o_ref):
    a = a_ref[...]
    a = a * _sigmoid(a)
    o_ref[...] = _dot_f32(a, w_ref[...]) + b_ref[...]


def _mods(cond, mod_w, mod_b):
    depth, d, n = mod_w.shape
    tn = 1536
    return pl.pallas_call(
        _mods_kernel,
        grid=(depth, n // tn),
        in_specs=[
            pl.BlockSpec((8, d), lambda l, j: (0, 0)),
            pl.BlockSpec((None, d, tn), lambda l, j: (l, 0, j)),
            pl.BlockSpec((None, 1, tn), lambda l, j: (l, 0, j)),
        ],
        out_specs=pl.BlockSpec((None, 8, tn), lambda l, j: (l, 0, j)),
        out_shape=jax.ShapeDtypeStruct((depth, 8, n), F32),
        compiler_params=_cparams("arbitrary", "arbitrary"),
        name="mods",
    )(cond, mod_w, mod_b.reshape(depth, 1, n))


def _inproj_kernel(x_ref, g_ref, mod_ref, w_ref, o_ref):
    d = x_ref.shape[-1]
    m = _mod_parts(mod_ref[...], d)
    h = _rms(x_ref[...], g_ref[...]) * (1.0 + m[1]) + m[0]
    o_ref[...] = _dot(h.astype(BF16), w_ref[...])


def _mod_spec(d):
    return pl.BlockSpec((None, 1, N_MOD * d), lambda b, i: (2 * b + jnp.minimum(i, 1), 0, 0))


def _inproj(xz, g, mods, w):
    bsz, s, d = xz.shape
    n = w.shape[1]
    return pl.pallas_call(
        _inproj_kernel,
        grid=(bsz, s // ROW_BLOCK),
        in_specs=[
            pl.BlockSpec((None, ROW_BLOCK, d), lambda b, i: (b, i, 0)),
            pl.BlockSpec((1, d), lambda b, i: (0, 0)),
            _mod_spec(d),
            pl.BlockSpec((d, n), lambda b, i: (0, 0)),
        ],
        out_specs=pl.BlockSpec((None, ROW_BLOCK, n), lambda b, i: (b, i, 0)),
        out_shape=jax.ShapeDtypeStruct((bsz, s, n), F32),
        compiler_params=_cparams("arbitrary", "arbitrary"),
        name="inproj",
    )(xz, g, mods, w)


def _gla_kernel(qk_ref, v_ref, r_ref, gw_ref, gb_ref, o_ref, st_ref, *, reverse):
    rows = qk_ref.shape[0]
    kw = qk_ref.shape[1] // 2
    n_chunks = rows // GLA_CHUNK
    dk = kw // GLA_HEADS
    dv = v_ref.shape[1] // GLA_HEADS

    @pl.when(pl.program_id(1) == 0)
    def _():
        st_ref[...] = jnp.zeros_like(st_ref)

    qk = qk_ref[...]
    q = qk[:, :kw] * (dk ** -0.5)
    k = qk[:, kw:]
    gl = _dot_f32(r_ref[...], gw_ref[...]) + gb_ref[...]
    lg = (jnp.minimum(gl, 0.0) - jnp.log(1.0 + jnp.exp(-jnp.abs(gl)))) * (1.0 / GLA_GATE_NORM)

    ri = lax.broadcasted_iota(jnp.int32, (rows, rows), 0)
    ci = lax.broadcasted_iota(jnp.int32, (rows, rows), 1)
    chunk_bits = GLA_CHUNK.bit_length() - 1
    same = jnp.right_shift(ri, chunk_bits) == jnp.right_shift(ci, chunk_bits)
    tri = (ci >= ri) if reverse else (ci <= ri)
    tri_m = jnp.where(same & tri, 1.0, 0.0).astype(BF16)
    blk_m = jnp.where(same, 1.0, 0.0).astype(BF16)
    lg3 = _split3(lg)
    cum = _dot(tri_m, lg3[0]) + _dot(tri_m, lg3[1]) + _dot(tri_m, lg3[2])
    tot = _dot(blk_m, lg3[0]) + _dot(blk_m, lg3[1]) + _dot(blk_m, lg3[2])

    q_dec = q * jnp.exp(cum)
    k_inv = (k * jnp.exp(-cum)).astype(BF16)
    k_end = k * jnp.exp(tot - cum)
    dl = jnp.exp(tot)
    v = v_ref[...].astype(BF16)

    r64 = lax.broadcasted_iota(jnp.int32, (GLA_CHUNK, GLA_CHUNK), 0)
    c64 = lax.broadcasted_iota(jnp.int32, (GLA_CHUNK, GLA_CHUNK), 1)
    att_mask = (c64 >= r64) if reverse else (c64 <= r64)
    lane = lax.broadcasted_iota(jnp.int32, (GLA_CHUNK, LANES), 1)
    sub = lax.broadcasted_iota(jnp.int32, (LANES, dv), 0)

    chunk_order = range(n_chunks - 1, -1, -1) if reverse else range(n_chunks)
    for c in chunk_order:
        r0 = c * GLA_CHUNK
        for p in range(kw // LANES):
            qd_p = q_dec[r0:r0 + GLA_CHUNK, p * LANES:(p + 1) * LANES]
            ki_p = k_inv[r0:r0 + GLA_CHUNK, p * LANES:(p + 1) * LANES]
            ke_t = k_end[r0:r0 + GLA_CHUNK, p * LANES:(p + 1) * LANES].T
            dl_t = dl[r0:r0 + GLA_CHUNK, p * LANES:(p + 1) * LANES].T[:, 0:1]
            for hh in range(LANES // dk):
                h = p * (LANES // dk) + hh
                in_head = (lane >= hh * dk) & (lane < (hh + 1) * dk)
                qd_h = jnp.where(in_head, qd_p, 0.0).astype(BF16)
                v_h = v[r0:r0 + GLA_CHUNK, h * dv:(h + 1) * dv]
                att = jnp.where(att_mask, _dot_nt(qd_h, ki_p), 0.0)
                s_h = st_ref[h]
                o = _dot(att.astype(BF16), v_h) + _dot(qd_h, s_h.astype(BF16))
                o_ref[r0:r0 + GLA_CHUNK, h * dv:(h + 1) * dv] = o
                head_rows = (sub >= hh * dk) & (sub < (hh + 1) * dk)
                upd = dl_t * s_h + _dot(ke_t.astype(BF16), v_h)
                st_ref[h] = jnp.where(head_rows, upd, 0.0)


def _gla(p_all, gw_ext, gb, *, reverse, kw, vw, col_v, col_r):
    bsz, s, _ = p_all.shape
    nblk = s // ROW_BLOCK
    if reverse:
        blk = lambda i: jnp.where(i == 0, 0, nblk - i)
    else:
        blk = lambda i: i
    return pl.pallas_call(
        functools.partial(_gla_kernel, reverse=reverse),
        grid=(bsz, nblk),
        in_specs=[
            pl.BlockSpec((None, ROW_BLOCK, 2 * kw), lambda b, i: (b, blk(i), 0)),
            pl.BlockSpec((None, ROW_BLOCK, vw), lambda b, i: (b, blk(i), col_v)),
            pl.BlockSpec((None, ROW_BLOCK, LANES), lambda b, i: (b, blk(i), col_r)),
            pl.BlockSpec((LANES, kw), lambda b, i: (0, 0)),
            pl.BlockSpec((1, kw), lambda b, i: (0, 0)),
        ],
        out_specs=pl.BlockSpec((None, ROW_BLOCK, vw), lambda b, i: (b, blk(i), 0)),
        out_shape=jax.ShapeDtypeStruct((bsz, s, vw), F32),
        scratch_shapes=[pltpu.VMEM((GLA_HEADS, LANES, vw // GLA_HEADS), F32)],
        compiler_params=_cparams("arbitrary", "arbitrary"),
        name="gla_bwd" if reverse else "gla_fwd",
    )(p_all, p_all, p_all, gw_ext, gb)


def _s5_kernel(u_ref, wsin_ref, m_ref, cout_ref, a_ref, y_ref, s_ref, h_ref, *, n_ctx_slabs):
    rows = u_ref.shape[0]
    n_slabs = rows // 8
    u = u_ref[...]
    s_ref[...] = _dot(u, wsin_ref[...])

    a = a_ref[...]
    af_re, af_im, ab_re, ab_im = a[0:1], a[1:2], a[2:3], a[3:4]
    low = lax.broadcasted_iota(jnp.int32, (8, LANES), 0) < 4

    def cmul(ar, ai, hr, hi):
        return ar * hr - ai * hi, ar * hi + ai * hr

    def slab_at(k):
        return pl.ds(pl.multiple_of(k * 8, 8), 8)

    def fwd_step(k, carry):
        hr, hi = carry
        rs = slab_at(k)
        sr, si = s_ref[rs, 0:LANES], s_ref[rs, LANES:2 * LANES]
        ahr, ahi = cmul(af_re, af_im, hr, hi)
        tr, ti = ahr + pltpu.roll(sr, 4, 0), ahi + pltpu.roll(si, 4, 0)
        outr, outi = jnp.where(low, hr, tr), jnp.where(low, hi, ti)
        h_ref[rs, 0:LANES] = outr
        h_ref[rs, LANES:2 * LANES] = outi
        nr, ni = cmul(af_re, af_im, outr, outi)
        nr, ni = nr + sr, ni + si
        return jnp.where(low, pltpu.roll(nr, 4, 0), nr), jnp.where(low, pltpu.roll(ni, 4, 0), ni)

    def bwd_step(k, carry):
        hr, hi = carry
        rs = slab_at(k)
        sr, si = s_ref[rs, 2 * LANES:3 * LANES], s_ref[rs, 3 * LANES:4 * LANES]
        ahr, ahi = cmul(ab_re, ab_im, hr, hi)
        tr, ti = ahr + pltpu.roll(sr, 4, 0), ahi + pltpu.roll(si, 4, 0)
        outr, outi = jnp.where(low, tr, hr), jnp.where(low, ti, hi)
        h_ref[rs, 2 * LANES:3 * LANES] = outr
        h_ref[rs, 3 * LANES:4 * LANES] = outi
        nr, ni = cmul(ab_re, ab_im, outr, outi)
        nr, ni = nr + sr, ni + si
        return jnp.where(low, nr, pltpu.roll(nr, 4, 0)), jnp.where(low, ni, pltpu.roll(ni, 4, 0))

    zero = (jnp.zeros((8, LANES), F32), jnp.zeros((8, LANES), F32))
    lax.fori_loop(0, n_slabs, fwd_step, zero)
    mid = lax.fori_loop(0, n_ctx_slabs, lambda j, c: bwd_step(n_ctx_slabs - 1 - j, c), zero)
    lax.fori_loop(0, n_slabs - n_ctx_slabs, lambda j, c: bwd_step(n_slabs - 1 - j, c), mid)

    hc = _dot(h_ref[...].astype(BF16), cout_ref[...])
    gw = m_ref.shape[-1]
    for g in range(m_ref.shape[0]):
        y_ref[:, g * gw:(g + 1) * gw] = _dot(u[:, g * gw:(g + 1) * gw], m_ref[g]) + hc[:, g * gw:(g + 1) * gw]


def _s5(u_chunks, wsin, m, cout, a16, n_ctx_chunks):
    rows, width = u_chunks.shape
    gw = m.shape[-1]
    n_pairs = width // (2 * gw)
    return pl.pallas_call(
        functools.partial(_s5_kernel, n_ctx_slabs=n_ctx_chunks // 2),
        grid=(n_pairs,),
        in_specs=[
            pl.BlockSpec((rows, 2 * gw), lambda j: (0, j)),
            pl.BlockSpec((None, 2 * gw, 4 * LANES), lambda j: (j, 0, 0)),
            pl.BlockSpec((2, gw, gw), lambda j: (j, 0, 0)),
            pl.BlockSpec((None, 4 * LANES, 2 * gw), lambda j: (j, 0, 0)),
            pl.BlockSpec((None, 4, LANES), lambda j: (j, 0, 0)),
        ],
        out_specs=pl.BlockSpec((rows, 2 * gw), lambda j: (0, j)),
        out_shape=jax.ShapeDtypeStruct((rows, width), F32),
        scratch_shapes=[pltpu.VMEM((rows, 4 * LANES), F32), pltpu.VMEM((rows, 4 * LANES), F32)],
        compiler_params=_cparams("arbitrary"),
        name="s5",
    )(u_chunks, wsin, m, cout, a16)


def _s5_matrices(lam_re, lam_im, log_dt, b_re, b_im, c_re, c_im):
    t = S5_CHUNK
    n_g, n_p, n_h = b_re.shape[1:]
    lam = lax.complex(lam_re, lam_im)
    dt = jnp.exp(log_dt)[..., None]
    a_bar = jnp.exp(lam * dt)
    bb = ((a_bar - 1.0) / lam)[..., None] * lax.complex(b_re, b_im)
    cm = lax.complex(c_re, c_im)
    steps = jnp.arange(t + 1, dtype=F32)
    apow = jnp.exp((lam * dt)[None] * steps[:, None, None, None])

    lag = jnp.real(jnp.einsum('dgop,ldgp,dgpi->dlgoi', cm, apow[:t], bb))
    ti = jnp.arange(t)[:, None]
    to = jnp.arange(t)[None, :]
    kf = jnp.where((to >= ti)[..., None, None, None], lag[0][jnp.clip(to - ti, 0, t - 1)], 0.0)
    kb = jnp.where((ti >= to)[..., None, None, None], lag[1][jnp.clip(ti - to, 0, t - 1)], 0.0)
    m = (kf + kb).transpose(2, 0, 4, 1, 3).reshape(n_g, t * n_h, t * n_h)

    sf = apow[t - 1 - jnp.arange(t), 0][..., None] * bb[0][None]
    sb = apow[jnp.arange(t), 1][..., None] * bb[1][None]
    def s_cols(z):
        return z.transpose(1, 0, 3, 2).reshape(n_g, t * n_h, n_p)
    sin = [s_cols(jnp.real(sf)), s_cols(jnp.imag(sf)), s_cols(jnp.real(sb)), s_cols(jnp.imag(sb))]

    ef = cm[0][None] * apow[1 + jnp.arange(t), 0][:, :, None, :]
    eb = cm[1][None] * apow[t - jnp.arange(t), 1][:, :, None, :]
    def c_rows(z):
        return z.transpose(1, 3, 0, 2).reshape(n_g, n_p, t * n_h)
    cout = [c_rows(jnp.real(ef)), -c_rows(jnp.imag(ef)), c_rows(jnp.real(eb)), -c_rows(jnp.imag(eb))]

    gw = t * n_h
    n_pairs = n_g // 2
    zs = jnp.zeros((n_pairs, gw, n_p), F32)
    wsin = jnp.concatenate([
        jnp.concatenate([jnp.concatenate([part[0::2], zs], -1) for part in sin], -1),
        jnp.concatenate([jnp.concatenate([zs, part[1::2]], -1) for part in sin], -1)], 1)
    zc = jnp.zeros((n_pairs, n_p, gw), F32)
    cpack = jnp.concatenate([
        jnp.concatenate([jnp.concatenate([part[0::2], zc], -1), jnp.concatenate([zc, part[1::2]], -1)], 1)
        for part in cout], 1)
    a16 = apow[t]
    def a_tile(z):
        return z.reshape(n_pairs, 2 * n_p)
    a_pack = jnp.stack([a_tile(jnp.real(a16[0])), a_tile(jnp.imag(a16[0])),
                        a_tile(jnp.real(a16[1])), a_tile(jnp.imag(a16[1]))], 1)
    return wsin.astype(BF16), m.astype(BF16), cpack.astype(BF16), a_pack


def _mix_epilogue(y, x, m, post_g, pre_g, rw, rb, xo_ref, f_ref, lg_ref):
    x_new = x + m[2] * _rms(y, post_g)
    xo_ref[...] = x_new
    f_in = _rms(x_new, pre_g) * (1.0 + m[4]) + m[3]
    f_ref[...] = f_in.astype(f_ref.dtype)
    lg_ref[...] = _dot_f32(f_in, rw) + rb


def _recout_kernel(of_ref, ob_ref, gate_ref, ys_ref, u_ref, x_ref, mod_ref, glag_ref, s5d_ref, gluw_ref,
                   glub_ref, wout_ref, post_ref, pre_ref, rw_ref, rb_ref, xo_ref, f_ref, lg_ref):
    d = x_ref.shape[-1]
    dv = glag_ref.shape[-1]
    gla_w = of_ref.shape[-1]
    m = _mod_parts(mod_ref[...], d)
    o = of_ref[...] + ob_ref[...]
    gate = gate_ref[...]
    parts = []
    for h in range(gla_w // dv):
        oh = _rms(o[:, h * dv:(h + 1) * dv], glag_ref[...])
        gh = gate[:, h * dv:(h + 1) * dv]
        parts.append((oh * (gh * _sigmoid(gh))).astype(BF16))
    a = jnp.concatenate(parts, axis=-1)
    u = u_ref[...]
    y = ys_ref[...] + s5d_ref[...] * u
    y = 0.5 * y * (1.0 + jnp.tanh(math.sqrt(2.0 / math.pi) * (y + 0.044715 * (y * y * y))))
    y = y * _sigmoid(_dot(y.astype(BF16), gluw_ref[...]) + glub_ref[...])
    out = _dot(a, wout_ref[0:gla_w, :]) + _dot(y.astype(BF16), wout_ref[gla_w:, :])
    _mix_epilogue(out, x_ref[...], m, post_ref[...], pre_ref[...], rw_ref[...], rb_ref[...],
                  xo_ref, f_ref, lg_ref)


def _recout(o_f, o_b, p_all, y_s5, xz, mods, gla_g, s5_d, glu_w, glu_b, w_out, post_g, pre_g, rw, rb,
            *, col_gate, col_u):
    bsz, s, d = xz.shape
    gla_w = o_f.shape[-1]
    s5_w = y_s5.shape[-1]
    row = lambda w, col=0: pl.BlockSpec((None, ROW_BLOCK, w), lambda b, i: (b, i, col))
    full = lambda a: pl.BlockSpec(a.shape, lambda b, i: (0,) * a.ndim)
    consts = [gla_g, s5_d, glu_w, glu_b, w_out, post_g, pre_g, rw, rb]
    return pl.pallas_call(
        _recout_kernel,
        grid=(bsz, s // ROW_BLOCK),
        in_specs=[row(gla_w), row(gla_w), row(gla_w, col_gate), row(s5_w), row(s5_w, col_u), row(d),
                  _mod_spec(d)] + [full(a) for a in consts],
        out_specs=[row(d), row(d), row(LANES)],
        out_shape=[jax.ShapeDtypeStruct((bsz, s, d), F32), jax.ShapeDtypeStruct((bsz, s, d), BF16),
                   jax.ShapeDtypeStruct((bsz, s, LANES), F32)],
        compiler_params=_cparams("arbitrary", "arbitrary"),
        name="recout",
    )(o_f, o_b, p_all, y_s5, p_all, xz, mods, *consts)


def _mlaproj_kernel(x_ref, g_ref, mod_ref, wd_ref, qn_ref, kvn_ref, wuq_ref, wukv_ref, cos_ref, sin_ref,
                    q_ref, k_ref, v_ref):
    d = x_ref.shape[-1]
    m = _mod_parts(mod_ref[...], d)
    h = _rms(x_ref[...], g_ref[...]) * (1.0 + m[1]) + m[0]
    down = _dot(h.astype(BF16), wd_ref[...])
    qr, kvr = MLA_Q_RANK, MLA_KV_RANK
    cq = _rms(down[:, :qr], qn_ref[...]).astype(BF16)
    ckv = _rms(down[:, qr:qr + kvr], kvn_ref[...]).astype(BF16)
    cos = cos_ref[...]
    sin = sin_ref[...]
    kr = down[:, qr + kvr:qr + kvr + LANES] * cos + down[:, qr + kvr + LANES:] * sin
    kr = kr.astype(BF16)
    kv = _dot(ckv, wukv_ref[...])
    nh = MLA_HEADS
    for hd in range(nh):
        k_ref[:, hd * MLA_QK_PAD:hd * MLA_QK_PAD + MLA_NOPE] = kv[:, hd * MLA_NOPE:(hd + 1) * MLA_NOPE].astype(BF16)
        k_ref[:, hd * MLA_QK_PAD + MLA_NOPE:(hd + 1) * MLA_QK_PAD] = kr
    v_ref[...] = kv[:, nh * MLA_NOPE:].astype(BF16)

    @pl.when(pl.program_id(1) > 0)
    def _():
        qa = _dot(cq, wuq_ref[...])
        scale = (MLA_NOPE + MLA_ROPE) ** -0.5 * LOG2E
        base_r = nh * MLA_NOPE
        base_s = base_r + nh * LANES
        for hd in range(nh):
            q_ref[:, hd * MLA_QK_PAD:hd * MLA_QK_PAD + MLA_NOPE] = (
                qa[:, hd * MLA_NOPE:(hd + 1) * MLA_NOPE] * scale).astype(BF16)
            rope = (qa[:, base_r + hd * LANES:base_r + (hd + 1) * LANES] * cos
                    + qa[:, base_s + hd * LANES:base_s + (hd + 1) * LANES] * sin)
            q_ref[:, hd * MLA_QK_PAD + MLA_NOPE:(hd + 1) * MLA_QK_PAD] = (rope * scale).astype(BF16)


def _mlaproj(xz, g, mods, wd, qn, kvn, wuq, wukv, cos, sin, n_ctx):
    bsz, s, d = xz.shape
    ctx_blocks = n_ctx // ROW_BLOCK
    full = lambda a: pl.BlockSpec(a.shape, lambda b, i: (0,) * a.ndim)
    qkw = MLA_HEADS * MLA_QK_PAD
    vw = MLA_HEADS * MLA_V
    return pl.pallas_call(
        _mlaproj_kernel,
        grid=(bsz, s // ROW_BLOCK),
        in_specs=[pl.BlockSpec((None, ROW_BLOCK, d), lambda b, i: (b, i, 0)), full(g), _mod_spec(d),
                  full(wd), full(qn), full(kvn), full(wuq), full(wukv),
                  pl.BlockSpec((ROW_BLOCK, LANES), lambda b, i: (i, 0)),
                  pl.BlockSpec((ROW_BLOCK, LANES), lambda b, i: (i, 0))],
        out_specs=[pl.BlockSpec((None, ROW_BLOCK, qkw), lambda b, i: (b, jnp.maximum(i - ctx_blocks, 0), 0)),
                   pl.BlockSpec((None, ROW_BLOCK, qkw), lambda b, i: (b, i, 0)),
                   pl.BlockSpec((None, ROW_BLOCK, vw), lambda b, i: (b, i, 0))],
        out_shape=[jax.ShapeDtypeStruct((bsz, s - n_ctx, qkw), BF16),
                   jax.ShapeDtypeStruct((bsz, s, qkw), BF16),
                   jax.ShapeDtypeStruct((bsz, s, vw), BF16)],
        compiler_params=_cparams("arbitrary", "arbitrary"),
        name="mlaproj",
    )(xz, g, mods, wd, qn, kvn, wuq, wukv, cos, sin)


def _attn_kernel(q_ref, k_ref, v_ref, o_ref):
    s = _dot_nt(q_ref[...], k_ref[...])
    mx = jnp.max(s, axis=-1, keepdims=True)
    p = jnp.exp2(s - mx)
    l = jnp.sum(p, axis=-1, keepdims=True)
    o = _dot(p.astype(BF16), v_ref[...])
    o_ref[...] = (o / l).astype(o_ref.dtype)


def _attention(q, k, v):
    bsz, lq, _ = q.shape
    sk = k.shape[1]
    return pl.pallas_call(
        _attn_kernel,
        grid=(bsz, MLA_HEADS, lq // ATTN_Q_BLOCK),
        in_specs=[pl.BlockSpec((None, ATTN_Q_BLOCK, MLA_QK_PAD), lambda b, h, i: (b, i, h)),
                  pl.BlockSpec((None, sk, MLA_QK_PAD), lambda b, h, i: (b, 0, h)),
                  pl.BlockSpec((None, sk, MLA_V), lambda b, h, i: (b, 0, h))],
        out_specs=pl.BlockSpec((None, ATTN_Q_BLOCK, MLA_V), lambda b, h, i: (b, i, h)),
        out_shape=jax.ShapeDtypeStruct((bsz, lq, MLA_HEADS * MLA_V), BF16),
        compiler_params=_cparams("arbitrary", "arbitrary", "arbitrary"),
        name="attention",
    )(q, k, v)


def _attnout_kernel(o_ref, x_ref, mod_ref, wo_ref, post_ref, pre_ref, rw_ref, rb_ref, xo_ref, f_ref, lg_ref):
    d = x_ref.shape[-1]
    m = _mod_parts(mod_ref[...], d)
    y = _dot(o_ref[...], wo_ref[...])
    _mix_epilogue(y, x_ref[...], m, post_ref[...], pre_ref[...], rw_ref[...], rb_ref[...], xo_ref, f_ref, lg_ref)


def _attnout(o, xz, mods, w_o, post_g, pre_g, rw, rb, n_ctx):
    bsz, lq, ow = o.shape
    d = xz.shape[-1]
    ctx_blocks = n_ctx // ROW_BLOCK
    row = lambda w: pl.BlockSpec((None, ROW_BLOCK, w), lambda b, i: (b, i, 0))
    full = lambda a: pl.BlockSpec(a.shape, lambda b, i: (0,) * a.ndim)
    consts = [w_o, post_g, pre_g, rw, rb]
    return pl.pallas_call(
        _attnout_kernel,
        grid=(bsz, lq // ROW_BLOCK),
        in_specs=[row(ow), pl.BlockSpec((None, ROW_BLOCK, d), lambda b, i: (b, i + ctx_blocks, 0)),
                  pl.BlockSpec((None, 1, N_MOD * d), lambda b, i: (2 * b + 1, 0, 0))] + [full(a) for a in consts],
        out_specs=[row(d), row(d), row(LANES)],
        out_shape=[jax.ShapeDtypeStruct((bsz, lq, d), F32), jax.ShapeDtypeStruct((bsz, lq, d), BF16),
                   jax.ShapeDtypeStruct((bsz, lq, LANES), F32)],
        compiler_params=_cparams("arbitrary", "arbitrary"),
        name="attnout",
    )(o, xz, mods, *consts)


def _experts_kernel(blk_e_ref, n_used_ref, x_ref, w1_ref, b1_ref, w2_ref, b2_ref, o_ref):
    i = pl.program_id(0)

    @pl.when(i < n_used_ref[0])
    def _():
        f = w2_ref.shape[0]
        gu = _dot(x_ref[...], w1_ref[...]) + b1_ref[...]
        gate = jnp.minimum(gu[:, :f], SWIGLU_LIMIT)
        up = jnp.clip(gu[:, f:], -SWIGLU_LIMIT, SWIGLU_LIMIT)
        act = (up + 1.0) * gate * _sigmoid(SWIGLU_ALPHA * gate)
        o_ref[...] = _dot(act.astype(BF16), w2_ref[...]) + b2_ref[...]

    @pl.when(i >= n_used_ref[0])
    def _():
        o_ref[...] = jnp.zeros_like(o_ref)


def _experts(rows, blk_e, n_used, w1, b1, w2, b2):
    n_rows, d = rows.shape
    n_e, _, f2 = w1.shape
    f = f2 // 2
    n_blocks = n_rows // MOE_ROWS
    grid_spec = pltpu.PrefetchScalarGridSpec(
        num_scalar_prefetch=2,
        grid=(n_blocks,),
        in_specs=[pl.BlockSpec((MOE_ROWS, d), lambda i, e, n: (i, 0)),
                  pl.BlockSpec((None, d, f2), lambda i, e, n: (e[i], 0, 0)),
                  pl.BlockSpec((None, 1, f2), lambda i, e, n: (e[i], 0, 0)),
                  pl.BlockSpec((None, f, d), lambda i, e, n: (e[i], 0, 0)),
                  pl.BlockSpec((None, 1, d), lambda i, e, n: (e[i], 0, 0))],
        out_specs=pl.BlockSpec((MOE_ROWS, d), lambda i, e, n: (i, 0)),
    )
    return pl.pallas_call(
        _experts_kernel,
        grid_spec=grid_spec,
        out_shape=jax.ShapeDtypeStruct((n_rows, d), F32),
        compiler_params=_cparams("arbitrary"),
        name="experts",
    )(blk_e, n_used, rows, w1, b1.reshape(n_e, 1, f2), w2, b2.reshape(n_e, 1, d))


def _moe(f_in, logits, w1, b1, w2, b2):
    n_tok, d = f_in.shape
    n_e = w1.shape[0]
    top_val, top_idx = lax.top_k(logits, TOP_K)
    gates = jax.nn.softmax(top_val, axis=-1)
    n_asg = n_tok * TOP_K
    flat_e = top_idx.reshape(-1)
    order = jnp.argsort(flat_e)
    sorted_e = flat_e[order]
    counts = jnp.bincount(flat_e, length=n_e)
    padded = (counts + MOE_ROWS - 1) // MOE_ROWS * MOE_ROWS
    pad_end = jnp.cumsum(padded)
    pad_start = pad_end - padded
    grp_start = jnp.cumsum(counts) - counts
    dest_sorted = (pad_start[sorted_e] + jnp.arange(n_asg) - grp_start[sorted_e]).astype(jnp.int32)
    n_blocks = -(-n_asg // MOE_ROWS) + n_e
    n_rows = n_blocks * MOE_ROWS
    row_tok = jnp.zeros((n_rows,), jnp.int32).at[dest_sorted].set((order // TOP_K).astype(jnp.int32))
    rows = f_in[row_tok]
    blk_e = jnp.minimum(jnp.searchsorted(pad_end, jnp.arange(n_blocks) * MOE_ROWS, side='right'),
                        n_e - 1).astype(jnp.int32)
    n_used = (pad_end[-1:] // MOE_ROWS).astype(jnp.int32)
    y_rows = _experts(rows, blk_e, n_used, w1, b1, w2, b2)
    dest = jnp.zeros((n_asg,), jnp.int32).at[order].set(dest_sorted)
    y_asg = y_rows[dest].reshape(n_tok, TOP_K, d)
    return jnp.sum(gates[:, :, None] * y_asg, axis=1)


def _ffnres_kernel(x_ref, f_ref, mod_ref, g_ref, o_ref):
    d = x_ref.shape[-1]
    m = _mod_parts(mod_ref[...], d)
    o_ref[...] = x_ref[...] + m[5] * _rms(f_ref[...], g_ref[...])


def _ffnres(x, f, mods, g, mod_spec):
    bsz, s, d = x.shape
    row = pl.BlockSpec((None, ROW_BLOCK, d), lambda b, i: (b, i, 0))
    return pl.pallas_call(
        _ffnres_kernel,
        grid=(bsz, s // ROW_BLOCK),
        in_specs=[row, row, mod_spec, pl.BlockSpec((1, d), lambda b, i: (0, 0))],
        out_specs=row,
        out_shape=jax.ShapeDtypeStruct((bsz, s, d), F32),
        compiler_params=_cparams("arbitrary", "arbitrary"),
        name="ffnres",
    )(x, f, mods, g)


def _rope_tables(seq_len, n_ctx):
    rows = seq_len // GRID_W
    row = jnp.broadcast_to(jnp.arange(rows, dtype=F32)[:, None], (rows, GRID_W)).reshape(-1)
    col = jnp.broadcast_to(jnp.arange(GRID_W, dtype=F32)[None, :], (rows, GRID_W)).reshape(-1)
    n_freq = MLA_ROPE // 4
    inv = ROPE_THETA ** (-jnp.arange(n_freq, dtype=F32) / n_freq)
    ang_r = row[:, None] * inv
    ang_c = col[:, None] * inv
    cr, sr, cc, sc = jnp.cos(ang_r), jnp.sin(ang_r), jnp.cos(ang_c), jnp.sin(ang_c)
    pad = jnp.zeros((seq_len, LANES - MLA_ROPE), F32)
    cos = jnp.concatenate([cr, cr, cc, cc, pad], axis=-1)
    sin = jnp.concatenate([-sr, sr, -sc, sc, pad], axis=-1)
    cos_ctx = jnp.concatenate([jnp.ones((n_ctx, MLA_ROPE), F32), jnp.zeros((n_ctx, LANES - MLA_ROPE), F32)], -1)
    return (jnp.concatenate([cos_ctx, cos], 0), jnp.concatenate([jnp.zeros((n_ctx, LANES), F32), sin], 0))


def _swap_quarters(w):
    q = MLA_ROPE // 4
    return jnp.concatenate([w[..., q:2 * q], w[..., 0:q], w[..., 3 * q:4 * q], w[..., 2 * q:3 * q]], axis=-1)


def _pad_lanes(w):
    return jnp.concatenate([w, jnp.zeros(w.shape[:-1] + (LANES - w.shape[-1],), w.dtype)], axis=-1)


def kernel(x, c, ctx, c_ctx, mod_w, mod_b, norm_mix_pre, norm_mix_post, norm_ffn_pre, norm_ffn_post, router_w, router_b, exp_w_in, exp_b_in, exp_w_out, exp_b_out, rec_w_in, rec_w_out, gla_gate_w, gla_gate_b, gla_norm, s5_lam_re, s5_lam_im, s5_log_dt, s5_b_re, s5_b_im, s5_c_re, s5_c_im, s5_d, s5_glu_w, s5_glu_b, mla_w_down, mla_q_norm, mla_kv_norm, mla_w_uq, mla_w_ukv, mla_w_o):
    bsz, seq, d = x.shape
    n_ctx = ctx.shape[1]
    s = n_ctx + seq
    n_e = router_w.shape[-1]
    assert n_ctx % ROW_BLOCK == 0 and seq % ATTN_Q_BLOCK == 0 and bsz == 4 and n_ctx == ROW_BLOCK

    cond = jnp.concatenate([c, c_ctx[None], jnp.zeros((8 - bsz - 1, d), F32)], axis=0)
    mods_all = _mods(cond, mod_w, mod_b)
    sel = jnp.stack([jnp.full((bsz,), bsz, jnp.int32), jnp.arange(bsz, dtype=jnp.int32)], 1).reshape(-1)

    xz = jnp.concatenate([ctx, x], axis=1)
    rw_pad = [_pad_lanes(router_w[i]) for i in range(2)]
    rb_pad = [_pad_lanes(router_b[i][None]) for i in range(2)]
    w1 = exp_w_in.astype(BF16)
    w2 = exp_w_out.astype(BF16)

    i = 0
    mods = mods_all[i][sel][:, None, :]
    kw = gla_gate_w.shape[-1]
    gla_w = 2 * kw
    s5_w = s5_d.shape[-1]
    w_in = rec_w_in[0]
    o_q, o_k, o_v, o_g, o_rf = 0, kw, 2 * kw, 2 * kw + gla_w, 2 * kw + 2 * gla_w
    o_u = o_rf + 2 * GLA_GATE_RANK
    w_ext = jnp.concatenate([w_in[:, o_q:o_v], w_in[:, o_v:o_g], w_in[:, o_g:o_rf], w_in[:, o_u:],
                             _pad_lanes(w_in[:, o_rf:o_u])], axis=1).astype(BF16)
    col_v, col_gate, col_u = (2 * kw) // gla_w, (2 * kw + gla_w) // gla_w, (2 * kw + 2 * gla_w) // s5_w
    col_r = (2 * kw + 2 * gla_w + s5_w) // LANES
    p_all = _inproj(xz, norm_mix_pre[i][None], mods, w_ext)

    zg = jnp.zeros((LANES - 2 * GLA_GATE_RANK, kw), F32)
    gw_f = jnp.concatenate([gla_gate_w[0, 0], jnp.zeros((GLA_GATE_RANK, kw), F32), zg], 0)
    gw_b = jnp.concatenate([jnp.zeros((GLA_GATE_RANK, kw), F32), gla_gate_w[0, 1], zg], 0)
    gla_kw = dict(kw=kw, vw=gla_w, col_v=col_v, col_r=col_r)
    o_f = _gla(p_all, gw_f, gla_gate_b[0, 0][None], reverse=False, **gla_kw)
    o_b = _gla(p_all, gw_b, gla_gate_b[0, 1][None], reverse=True, **gla_kw)

    n_g = s5_w // S5_GROUP_CH
    n_chunks = s // S5_CHUNK
    u = p_all[:, :, col_u * s5_w:(col_u + 1) * s5_w]
    u_chunks = u.reshape(bsz, n_chunks, S5_CHUNK, n_g, S5_GROUP_CH).transpose(1, 0, 3, 2, 4)
    u_chunks = u_chunks.reshape(n_chunks * bsz, n_g * S5_CHUNK * S5_GROUP_CH).astype(BF16)
    wsin, m_toe, cpack, a_pack = _s5_matrices(s5_lam_re[0], s5_lam_im[0], s5_log_dt[0], s5_b_re[0], s5_b_im[0],
                                              s5_c_re[0], s5_c_im[0])
    y_chunks = _s5(u_chunks, wsin, m_toe, cpack, a_pack, n_ctx // S5_CHUNK)
    y_s5 = y_chunks.reshape(n_chunks, bsz, n_g, S5_CHUNK, S5_GROUP_CH).transpose(1, 0, 3, 2, 4).reshape(bsz, s, s5_w)

    xz, f_in, logits = _recout(
        o_f, o_b, p_all, y_s5, xz, mods, gla_norm[0][None], s5_d[0][None], s5_glu_w[0].astype(BF16),
        s5_glu_b[0][None], rec_w_out[0].astype(BF16), norm_mix_post[i][None], norm_ffn_pre[i][None],
        rw_pad[i], rb_pad[i], col_gate=col_gate, col_u=col_u)
    f = _moe(f_in.reshape(bsz * s, d), logits.reshape(bsz * s, LANES)[:, :n_e], w1[i], exp_b_in[i], w2[i],
             exp_b_out[i])
    xz = _ffnres(xz, f.reshape(bsz, s, d), mods, norm_ffn_post[i][None], _mod_spec(d))

    i = 1
    mods = mods_all[i][sel][:, None, :]
    qr, kvr = MLA_Q_RANK, MLA_KV_RANK
    wd = mla_w_down[0]
    wd_ext = jnp.concatenate([wd[:, :qr + kvr], _pad_lanes(wd[:, qr + kvr:]),
                              _pad_lanes(_swap_quarters(wd[:, qr + kvr:]))], axis=1).astype(BF16)
    wuq = mla_w_uq[0].reshape(qr, MLA_HEADS, MLA_NOPE + MLA_ROPE)
    wuq_ext = jnp.concatenate([
        wuq[:, :, :MLA_NOPE].reshape(qr, -1),
        _pad_lanes(wuq[:, :, MLA_NOPE:]).reshape(qr, -1),
        _pad_lanes(_swap_quarters(wuq[:, :, MLA_NOPE:])).reshape(qr, -1)], axis=1).astype(BF16)
    wukv = mla_w_ukv[0].reshape(kvr, MLA_HEADS, MLA_NOPE + MLA_V)
    wukv_ext = jnp.concatenate([wukv[:, :, :MLA_NOPE].reshape(kvr, -1), wukv[:, :, MLA_NOPE:].reshape(kvr, -1)],
                               axis=1).astype(BF16)
    cos, sin = _rope_tables(seq, n_ctx)
    q, k, v = _mlaproj(xz, norm_mix_pre[i][None], mods, wd_ext, mla_q_norm[0][None], mla_kv_norm[0][None],
                       wuq_ext, wukv_ext, cos, sin, n_ctx)
    o = _attention(q, k, v)
    x1, f_in, logits = _attnout(o, xz, mods, mla_w_o[0].astype(BF16), norm_mix_post[i][None],
                                norm_ffn_pre[i][None], rw_pad[i], rb_pad[i], n_ctx)
    f = _moe(f_in.reshape(bsz * seq, d), logits.reshape(bsz * seq, LANES)[:, :n_e], w1[i], exp_b_in[i], w2[i],
             exp_b_out[i])
    lat_mod = pl.BlockSpec((None, 1, N_MOD * d), lambda b, j: (2 * b + 1, 0, 0))
    return _ffnres(x1, f.reshape(bsz, seq, d), mods, norm_ffn_post[i][None], lat_mod)
```
